```python
import math
import jax, jax.numpy as jnp
from jax import lax
import numpy as np

D_MODEL = 2048
BATCH = 2
SEQ = 4096
DEPTH = 4
DEC_BATCH = 128
DEC_SEQ = 1
PAST_LEN = 8192
PAGE_SIZE = 128

N_ATTN_LAYERS = (DEPTH + 1) // 2
N_DELTA_LAYERS = DEPTH // 2
MLA_HEADS = 8
Q_LORA = 512
KV_LORA = 512
MLA_NOPE = 128
MLA_ROPE = 64
MLA_V = 128
MLA_SCALE = (MLA_NOPE + MLA_ROPE) ** -0.5
DIFF_HEADS = 8
DIFF_KV_HEADS = 2
DIFF_GROUP = DIFF_HEADS // DIFF_KV_HEADS
DIFF_HEAD_DIM = 64
DIFF_V = 2 * DIFF_HEAD_DIM
DIFF_SCALE = DIFF_HEAD_DIM ** -0.5
ATTN_IN = (Q_LORA + KV_LORA + MLA_ROPE + DIFF_HEADS * 2 * DIFF_HEAD_DIM
           + DIFF_KV_HEADS * 2 * DIFF_HEAD_DIM + DIFF_KV_HEADS * DIFF_V)
ATTN_MIX = MLA_HEADS * MLA_V + DIFF_HEADS * DIFF_V
Q_BLOCK = 128
DELTA_QK_HEADS = 16
DELTA_V_HEADS = 32
DELTA_DK = 128
DELTA_DV = 128
DELTA_QK_DIM = DELTA_QK_HEADS * DELTA_DK
DELTA_V_DIM = DELTA_V_HEADS * DELTA_DV
CONV_WIDTH = 4
CONV_DIM = 2 * DELTA_QK_DIM + DELTA_V_DIM
DELTA_IN = CONV_DIM + DELTA_V_DIM + 2 * DELTA_V_HEADS
DELTA_CHUNK = 64
D_FF = ((8 * D_MODEL + 3 * 256 - 1) // (3 * 256)) * 256
ROPE_THETA = 10000.0
EPS = 1e-6
NEG_INF = -1e30

kernel_name = 'hybrid_mla_diffattn_gated_deltanet_step'


def rmsnorm(x, g):
    xf = x.astype(jnp.float32)
    y = xf * lax.rsqrt(jnp.mean(xf * xf, axis=-1, keepdims=True) + EPS)
    return (y * g.astype(jnp.float32)).astype(x.dtype)


def l2norm(x):
    xf = x.astype(jnp.float32)
    return (xf * lax.rsqrt(jnp.sum(xf * xf, axis=-1, keepdims=True) + EPS)).astype(x.dtype)


def rope(x, pos):
    d = x.shape[-1]
    inv = jnp.power(jnp.float32(ROPE_THETA), -jnp.arange(0, d, 2, dtype=jnp.float32) / d)
    ang = pos.astype(jnp.float32)[:, None] * inv[None, :]
    shape = (1, pos.shape[0]) + (1,) * (x.ndim - 3) + (d // 2,)
    cos, sin = jnp.cos(ang).reshape(shape), jnp.sin(ang).reshape(shape)
    xf = x.astype(jnp.float32)
    x1, x2 = xf[..., :d // 2], xf[..., d // 2:]
    return jnp.concatenate([x1 * cos - x2 * sin, x1 * sin + x2 * cos], axis=-1).astype(x.dtype)


def swiglu(x, wg, wu, wd):
    return (jax.nn.silu(x @ wg) * (x @ wu)) @ wd


def sandwich_ffn(h, mix, g_post, g_fpre, g_fpost, wg, wu, wd):
    h = h + rmsnorm(mix, g_post)
    return h + rmsnorm(swiglu(rmsnorm(h, g_fpre), wg, wu, wd), g_fpost)


def attn_project(hn, pos, w_in, g_q, g_kv, w_q_up, w_kv_up):
    b, s, _ = hn.shape
    u = hn @ w_in
    o1 = Q_LORA
    o2 = o1 + KV_LORA
    o3 = o2 + MLA_ROPE
    o4 = o3 + DIFF_HEADS * 2 * DIFF_HEAD_DIM
    o5 = o4 + DIFF_KV_HEADS * 2 * DIFF_HEAD_DIM
    uq, ukv, ukr = u[..., :o1], u[..., o1:o2], u[..., o2:o3]
    udq, udk, udv = u[..., o3:o4], u[..., o4:o5], u[..., o5:]
    q = (rmsnorm(uq, g_q) @ w_q_up).reshape(b, s, MLA_HEADS, MLA_NOPE + MLA_ROPE)
    q_rope = rope(q[..., MLA_NOPE:], pos)
    w_uk = w_kv_up.reshape(KV_LORA, MLA_HEADS, MLA_NOPE + MLA_V)[..., :MLA_NOPE]
    q_abs = jnp.einsum('bshn,chn->bshc', q[..., :MLA_NOPE], w_uk)
    c = rmsnorm(ukv, g_kv)
    kr = rope(ukr, pos)
    dq = rope(udq.reshape(b, s, DIFF_HEADS * 2, DIFF_HEAD_DIM), pos).reshape(
        b, s, DIFF_KV_HEADS, DIFF_GROUP, 2, DIFF_HEAD_DIM)
    dk = rope(udk.reshape(b, s, DIFF_KV_HEADS * 2, DIFF_HEAD_DIM), pos).reshape(
        b, s, DIFF_KV_HEADS, 2, DIFF_HEAD_DIM)
    dv = udv.reshape(b, s, DIFF_KV_HEADS, DIFF_V)
    return q_abs, q_rope, c, kr, dq, dk, dv


def mla_core(q_abs, q_rope, c, kr, qpos, kpos):
    sc = (jnp.einsum('bshc,btc->bhst', q_abs, c, preferred_element_type=jnp.float32)
          + jnp.einsum('bshr,btr->bhst', q_rope, kr, preferred_element_type=jnp.float32)) * MLA_SCALE
    sc = jnp.where(kpos[None, :] <= qpos[:, None], sc, NEG_INF)
    p = jax.nn.softmax(sc, axis=-1).astype(c.dtype)
    return jnp.einsum('bhst,btc->bshc', p, c)


def diff_core(dq, dk, dv, lam, qpos, kpos):
    sc = jnp.einsum('bskgpd,btkpd->bkgpst', dq, dk, preferred_element_type=jnp.float32) * DIFF_SCALE
    sc = jnp.where(kpos[None, :] <= qpos[:, None], sc, NEG_INF)
    p = jax.nn.softmax(sc, axis=-1)
    w = (p[:, :, :, 0] - lam * p[:, :, :, 1]).astype(dv.dtype)
    return jnp.einsum('bkgst,btkv->bskgv', w, dv)


def diff_lambda(lq1, lk1, lq2, lk2, layer):
    lam_init = 0.8 - 0.6 * math.exp(-0.3 * layer)
    f = jnp.float32
    lam = (jnp.exp(jnp.sum(lq1.astype(f) * lk1.astype(f)))
           - jnp.exp(jnp.sum(lq2.astype(f) * lk2.astype(f))) + lam_init)
    return lam, lam_init


def prompt_attention(q_abs, q_rope, c, kr, dq, dk, dv, lam):
    b, s = q_abs.shape[:2]
    nb = s // Q_BLOCK
    kpos = jnp.arange(s, dtype=jnp.int32)

    def to_blocks(t):
        return jnp.moveaxis(t.reshape((b, nb, Q_BLOCK) + t.shape[2:]), 1, 0)

    def from_blocks(t):
        return jnp.moveaxis(t, 0, 1).reshape((b, s) + t.shape[3:])

    def body(args):
        qa, qr, qd, qpos = args
        return (mla_core(qa, qr, c, kr, qpos, kpos), diff_core(qd, dk, dv, lam, qpos, kpos))

    o_lat, o_diff = lax.map(body, (to_blocks(q_abs), to_blocks(q_rope), to_blocks(dq),
                                   kpos.reshape(nb, Q_BLOCK)))
    return from_blocks(o_lat), from_blocks(o_diff)


def sample_attention(q_abs, q_rope, c, kr, dq, dk, dv, lam, page_table,
                     cache_lat, cache_kr, cache_dk, cache_dv, la):
    n_new = q_abs.shape[1]
    past = page_table.shape[1] * PAGE_SIZE
    qpos = past + jnp.arange(n_new, dtype=jnp.int32)
    kpos = jnp.arange(past + n_new, dtype=jnp.int32)

    def rows(cache, pages, new):
        old = cache[la, pages]
        old = old.reshape((1, past) + old.shape[2:])
        return jnp.concatenate([old, new[None].astype(old.dtype)], axis=1)

    def body(args):
        qa, qr, qd, c1, kr1, dk1, dv1, pages = args
        o_lat = mla_core(qa[None], qr[None], rows(cache_lat, pages, c1),
                         rows(cache_kr, pages, kr1), qpos, kpos)
        o_diff = diff_core(qd[None], rows(cache_dk, pages, dk1), rows(cache_dv, pages, dv1),
                           lam, qpos, kpos)
        return o_lat[0], o_diff[0]

    return lax.map(body, (q_abs, q_rope, dq, c, kr, dk, dv, page_table))


def attn_output(o_lat, o_diff, lam_init, w_kv_up, g_sub, w_out):
    b, s = o_lat.shape[:2]
    w_uv = w_kv_up.reshape(KV_LORA, MLA_HEADS, MLA_NOPE + MLA_V)[..., MLA_NOPE:]
    o_mla = jnp.einsum('bshc,chv->bshv', o_lat, w_uv).reshape(b, s, MLA_HEADS * MLA_V)
    o_d = rmsnorm(o_diff.reshape(b, s, DIFF_HEADS, DIFF_V), g_sub) * (1.0 - lam_init)
    o_d = o_d.reshape(b, s, DIFF_HEADS * DIFF_V)
    return jnp.concatenate([o_mla, o_d], axis=-1) @ w_out


def causal_conv(x, buf, w):
    xp = jnp.concatenate([buf.astype(x.dtype), x], axis=1)
    y = lax.conv_general_dilated(xp, w.astype(x.dtype)[:, None, :], window_strides=(1,), padding='VALID',
                                 dimension_numbers=('NHC', 'HIO', 'NHC'),
                                 feature_group_count=x.shape[-1])
    return jax.nn.silu(y), xp[:, -(CONV_WIDTH - 1):]


def gated_delta_chunked(q, k, v, g, beta, s0):
    b, s, _, _ = q.shape
    csz = min(DELTA_CHUNK, s)
    n = -(-s // csz)
    pad = n * csz - s
    f32 = jnp.float32

    def prep(t):
        t = jnp.pad(t.astype(f32), [(0, 0), (0, pad)] + [(0, 0)] * (t.ndim - 2))
        return jnp.moveaxis(t.reshape((b, n, csz) + t.shape[2:]), 3, 1)

    q, k, v, g, beta = prep(q), prep(k), prep(v), prep(g), prep(beta)
    gc = jnp.cumsum(g, axis=-1)
    incl = jnp.tril(jnp.ones((csz, csz), dtype=bool))
    strict = jnp.tril(jnp.ones((csz, csz), dtype=bool), -1)
    gdiff = gc[..., :, None] - gc[..., None, :]
    decay = jnp.where(incl, jnp.exp(jnp.where(incl, gdiff, 0.0)), 0.0)
    kb = k * beta[..., None]
    lmat = jnp.where(strict, jnp.einsum('bhnid,bhnjd->bhnij', kb, k) * decay, 0.0)
    value = lax.linalg.triangular_solve(lmat, v * beta[..., None], left_side=True, lower=True,
                                        unit_diagonal=True)
    kcd = lax.linalg.triangular_solve(lmat, kb * jnp.exp(gc)[..., None], left_side=True, lower=True,
                                      unit_diagonal=True)
    att = jnp.einsum('bhnid,bhnjd->bhnij', q, k) * decay
    q_dec = q * jnp.exp(gc)[..., None]
    k_dec = k * jnp.exp(gc[..., -1:] - gc)[..., None]
    g_last = jnp.exp(gc[..., -1])

    def step(st, xs):
        kcd_i, val_i, qd_i, att_i, kd_i, gl_i = xs
        v_new = val_i - jnp.einsum('bhcd,bhde->bhce', kcd_i, st)
        o = jnp.einsum('bhcd,bhde->bhce', qd_i, st) + jnp.einsum('bhcj,bhje->bhce', att_i, v_new)
        st = st * gl_i[..., None, None] + jnp.einsum('bhcd,bhce->bhde', kd_i, v_new)
        return st, o

    xs = tuple(jnp.moveaxis(t, 2, 0) for t in (kcd, value, q_dec, att, k_dec, g_last))
    st, o = lax.scan(step, s0.astype(f32), xs)
    o = o.transpose(1, 0, 3, 2, 4).reshape(b, n * csz, o.shape[2], o.shape[4])[:, :s]
    return o.astype(v.dtype), st.astype(s0.dtype)


def delta_mixer(hn, conv_buf, s0, w_in, w_conv, a_log, dt_bias, g_out, w_out):
    b, s, _ = hn.shape
    u = hn @ w_in
    o1 = CONV_DIM
    o2 = o1 + DELTA_V_DIM
    o3 = o2 + DELTA_V_HEADS
    qkv, new_buf = causal_conv(u[..., :o1], conv_buf, w_conv)
    z, bb, aa = u[..., o1:o2], u[..., o2:o3], u[..., o3:]
    rep = DELTA_V_HEADS // DELTA_QK_HEADS
    q = l2norm(qkv[..., :DELTA_QK_DIM].reshape(b, s, DELTA_QK_HEADS, DELTA_DK)) * (DELTA_DK ** -0.5)
    k = l2norm(qkv[..., DELTA_QK_DIM:2 * DELTA_QK_DIM].reshape(b, s, DELTA_QK_HEADS, DELTA_DK))
    q, k = jnp.repeat(q, rep, axis=2), jnp.repeat(k, rep, axis=2)
    v = qkv[..., 2 * DELTA_QK_DIM:].reshape(b, s, DELTA_V_HEADS, DELTA_DV)
    beta = jax.nn.sigmoid(bb.astype(jnp.float32))
    g = -jnp.exp(a_log.astype(jnp.float32)) * jax.nn.softplus(aa.astype(jnp.float32)
                                                             + dt_bias.astype(jnp.float32))
    o, st = gated_delta_chunked(q, k, v, g, beta, s0)
    o = rmsnorm(o, g_out) * jax.nn.silu(z.reshape(b, s, DELTA_V_HEADS, DELTA_DV))
    return o.reshape(b, s, DELTA_V_DIM) @ w_out, new_buf, st


def setup_inputs(seed: int = 0) -> dict:
    key = jax.random.key(seed)
    ks = iter(jax.random.split(key, 48))
    f32 = jnp.float32

    def nrm(shape, scale=1.0):
        return jax.random.normal(next(ks), shape, f32) * scale

    def gain(shape):
        return 1.0 + nrm(shape, 0.05)

    na, nd = N_ATTN_LAYERS, N_DELTA_LAYERS
    n_pages = PAST_LEN // PAGE_SIZE
    n_used = DEC_BATCH * n_pages
    n_pool = n_used + max(1, n_used // 4)
    x_prompt = nrm((BATCH, SEQ, D_MODEL))
    x_sample = nrm((DEC_BATCH, DEC_SEQ, D_MODEL))
    cache_mla_latent = nrm((na, n_pool, PAGE_SIZE, KV_LORA))
    cache_mla_krope = nrm((na, n_pool, PAGE_SIZE, MLA_ROPE))
    cache_diff_k = nrm((na, n_pool, PAGE_SIZE, DIFF_KV_HEADS, 2, DIFF_HEAD_DIM))
    cache_diff_v = nrm((na, n_pool, PAGE_SIZE, DIFF_KV_HEADS, DIFF_V))
    state_delta = nrm((nd, DEC_BATCH, DELTA_V_HEADS, DELTA_DK, DELTA_DV), 0.3)
    state_conv = nrm((nd, DEC_BATCH, CONV_WIDTH - 1, CONV_DIM))
    page_table = jax.random.permutation(next(ks), n_pool)[:n_used].reshape(DEC_BATCH, n_pages).astype(jnp.int32)
    dt = jnp.exp(jax.random.uniform(next(ks), (nd, DELTA_V_HEADS), f32, math.log(1e-3), math.log(1e-1)))
    return {
        'x_prompt': x_prompt, 'x_sample': x_sample,
        'cache_mla_latent': cache_mla_latent, 'cache_mla_krope': cache_mla_krope,
        'cache_diff_k': cache_diff_k, 'cache_diff_v': cache_diff_v,
        'state_delta': state_delta, 'state_conv': state_conv, 'page_table': page_table,
        'g_mix_pre': gain((DEPTH, D_MODEL)), 'g_mix_post': gain((DEPTH, D_MODEL)),
        'g_ffn_pre': gain((DEPTH, D_MODEL)), 'g_ffn_post': gain((DEPTH, D_MODEL)),
        'w_in_attn': nrm((na, D_MODEL, ATTN_IN), D_MODEL ** -0.5),
        'g_q_lat': gain((na, Q_LORA)), 'g_kv_lat': gain((na, KV_LORA)),
        'w_q_up': nrm((na, Q_LORA, MLA_HEADS * (MLA_NOPE + MLA_ROPE)), Q_LORA ** -0.5),
        'w_kv_up': nrm((na, KV_LORA, MLA_HEADS * (MLA_NOPE + MLA_V)), KV_LORA ** -0.5),
        'lambda_q1': nrm((na, DIFF_HEAD_DIM), 0.1), 'lambda_k1': nrm((na, DIFF_HEAD_DIM), 0.1),
        'lambda_q2': nrm((na, DIFF_HEAD_DIM), 0.1), 'lambda_k2': nrm((na, DIFF_HEAD_DIM), 0.1),
        'g_diff_sub': gain((na, DIFF_V)),
        'w_out_attn': nrm((na, ATTN_MIX, D_MODEL), ATTN_MIX ** -0.5),
        'w_in_delta': nrm((nd, D_MODEL, DELTA_IN), D_MODEL ** -0.5),
        'w_conv': nrm((nd, CONV_WIDTH, CONV_DIM), CONV_WIDTH ** -0.5),
        'a_log': jnp.log(jax.random.uniform(next(ks), (nd, DELTA_V_HEADS), f32, 1.0, 16.0)),
        'dt_bias': dt + jnp.log(-jnp.expm1(-dt)),
        'g_delta_out': gain((nd, DELTA_DV)),
        'w_out_delta': nrm((nd, DELTA_V_DIM, D_MODEL), DELTA_V_DIM ** -0.5),
        'w_gate': nrm((DEPTH, D_MODEL, D_FF), D_MODEL ** -0.5),
        'w_up': nrm((DEPTH, D_MODEL, D_FF), D_MODEL ** -0.5),
        'w_down': nrm((DEPTH, D_FF, D_MODEL), D_FF ** -0.5),
    }


def reference(x_prompt, x_sample, cache_mla_latent, cache_mla_krope, cache_diff_k, cache_diff_v,
              state_delta, state_conv, page_table, g_mix_pre, g_mix_post, g_ffn_pre, g_ffn_post,
              w_in_attn, g_q_lat, g_kv_lat, w_q_up, w_kv_up, lambda_q1, lambda_k1, lambda_q2, lambda_k2,
              g_diff_sub, w_out_attn, w_in_delta, w_conv, a_log, dt_bias, g_delta_out, w_out_delta,
              w_gate, w_up, w_down):
    past_len = page_table.shape[1] * PAGE_SIZE
    pos_p = jnp.arange(x_prompt.shape[1], dtype=jnp.int32)
    pos_s = past_len + jnp.arange(x_sample.shape[1], dtype=jnp.int32)
    h_p, h_s = x_prompt, x_sample
    lat_p, lat_s, kr_p, kr_s, dk_p, dk_s, dv_p, dv_s = [], [], [], [], [], [], [], []
    sd_p, sd_s, cv_p, cv_s = [], [], [], []
    for l in range(DEPTH):
        if l % 2 == 0:
            la = l // 2
            lam, lam_init = diff_lambda(lambda_q1[la], lambda_k1[la], lambda_q2[la], lambda_k2[la], l)
            w_proj = (w_in_attn[la], g_q_lat[la], g_kv_lat[la], w_q_up[la], w_kv_up[la])
            w_outp = (w_kv_up[la], g_diff_sub[la], w_out_attn[la])
            pp = attn_project(rmsnorm(h_p, g_mix_pre[l]), pos_p, *w_proj)
            o_lat, o_diff = prompt_attention(*pp, lam)
            mix_p = attn_output(o_lat, o_diff, lam_init, *w_outp)
            ps = attn_project(rmsnorm(h_s, g_mix_pre[l]), pos_s, *w_proj)
            o_lat_s, o_diff_s = sample_attention(*ps, lam, page_table, cache_mla_latent, cache_mla_krope,
                                                 cache_diff_k, cache_diff_v, la)
            mix_s = attn_output(o_lat_s, o_diff_s, lam_init, *w_outp)
            lat_p.append(pp[2]); kr_p.append(pp[3]); dk_p.append(pp[5]); dv_p.append(pp[6])
            lat_s.append(ps[2]); kr_s.append(ps[3]); dk_s.append(ps[5]); dv_s.append(ps[6])
        else:
            ld = l // 2
            w_d = (w_in_delta[ld], w_conv[ld], a_log[ld], dt_bias[ld], g_delta_out[ld], w_out_delta[ld])
            b = h_p.shape[0]
            buf0 = jnp.zeros((b, CONV_WIDTH - 1, CONV_DIM), h_p.dtype)
            st0 = jnp.zeros((b, DELTA_V_HEADS, DELTA_DK, DELTA_DV), h_p.dtype)
            mix_p, buf_p, st_p = delta_mixer(rmsnorm(h_p, g_mix_pre[l]), buf0, st0, *w_d)
            mix_s, buf_s, st_s = delta_mixer(rmsnorm(h_s, g_mix_pre[l]), state_conv[ld], state_delta[ld], *w_d)
            sd_p.append(st_p); sd_s.append(st_s); cv_p.append(buf_p); cv_s.append(buf_s)
        ffn_w = (g_mix_post[l], g_ffn_pre[l], g_ffn_post[l], w_gate[l], w_up[l], w_down[l])
        h_p = sandwich_ffn(h_p, mix_p, *ffn_w)
        h_s = sandwich_ffn(h_s, mix_s, *ffn_w)
    return (h_p, h_s,
            jnp.stack(lat_p), jnp.stack(lat_s), jnp.stack(kr_p), jnp.stack(kr_s),
            jnp.stack(dk_p), jnp.stack(dk_s), jnp.stack(dv_p), jnp.stack(dv_s),
            jnp.stack(sd_p), jnp.stack(sd_s), jnp.stack(cv_p), jnp.stack(cv_s))
```

```python
import functools
import math

import jax
import jax.numpy as jnp
from jax import lax
from jax.experimental import pallas as pl
from jax.experimental.pallas import tpu as pltpu

F32 = jnp.float32
BF16 = jnp.bfloat16

MLA_HEADS = 8
Q_LORA = 512
KV_LORA = 512
MLA_NOPE = 128
MLA_ROPE = 64
MLA_V = 128
MLA_SCALE = (MLA_NOPE + MLA_ROPE) ** -0.5
DIFF_HEADS = 8
DIFF_KV_HEADS = 2
DIFF_GROUP = DIFF_HEADS // DIFF_KV_HEADS
DIFF_HEAD_DIM = 64
DIFF_V = 2 * DIFF_HEAD_DIM
DIFF_SCALE = DIFF_HEAD_DIM ** -0.5
DIFF_Q_DIM = DIFF_HEADS * 2 * DIFF_HEAD_DIM
DIFF_K_DIM = DIFF_KV_HEADS * 2 * DIFF_HEAD_DIM
DIFF_V_DIM = DIFF_KV_HEADS * DIFF_V
ATTN_MIX = MLA_HEADS * MLA_V + DIFF_HEADS * DIFF_V
PAGE_SIZE = 128
Q_BLOCK = 128
DELTA_QK_HEADS = 16
DELTA_V_HEADS = 32
DELTA_DK = 128
DELTA_DV = 128
DELTA_QK_DIM = DELTA_QK_HEADS * DELTA_DK
DELTA_V_DIM = DELTA_V_HEADS * DELTA_DV
CONV_WIDTH = 4
CONV_DIM = 2 * DELTA_QK_DIM + DELTA_V_DIM
DELTA_CHUNK = 64
ROPE_THETA = 10000.0
EPS = 1e-6
NEG_INF = -1e30

V7X_VMEM_BYTES = 64 * 1024 * 1024
VMEM_LIMIT = V7X_VMEM_BYTES - 12 * 1024 * 1024
LANES = 128

A_UQ = 0
A_UKV = A_UQ + Q_LORA
A_UDQ = A_UKV + KV_LORA
A_UDK = A_UDQ + DIFF_Q_DIM
A_UDV = A_UDK + DIFF_K_DIM
A_UKR = A_UDV + DIFF_V_DIM
ATTN_IN_PAD = 2816
ATTN_IN_TN = 256

DELTA_HG = 4
STEP_HB = 8
SAMPLE_PAGES = 8
ATTN_TK = 512


def _params(sem):
    return pltpu.CompilerParams(dimension_semantics=sem, vmem_limit_bytes=VMEM_LIMIT)


def _row_tile(n, target, mult=16):
    best = None
    for t in range(mult, min(n, target) + 1, mult):
        if n % t == 0:
            best = t
    return best if best is not None else n


def _dot(a, b):
    return jnp.dot(a, b, preferred_element_type=F32)


def _dot_nt(a, b):
    return lax.dot_general(a, b, (((1,), (1,)), ((), ())), preferred_element_type=F32)


def _bf(x):
    return x.astype(BF16)


def _split3(x):
    hi = _bf(x)
    r = x - hi.astype(F32)
    mid = _bf(r)
    lo = _bf(r - mid.astype(F32))
    return hi, mid, lo


def _dot_hi(a, b):
    a0, a1, _ = _split3(a)
    b0, b1, _ = _split3(b)
    return _dot(a0, b0) + (_dot(a0, b1) + _dot(a1, b0))


def _rms(x, g):
    return x * lax.rsqrt(jnp.mean(x * x, axis=-1, keepdims=True) + EPS) * g


def _silu(x):
    return x * jax.nn.sigmoid(x)


def _norm_kernel(x_ref, g_ref, o_ref):
    o_ref[...] = _rms(x_ref[...], g_ref[...]).astype(o_ref.dtype)


def _norm_bf16(x, g, tm):
    t, d = x.shape
    return pl.pallas_call(
        _norm_kernel, grid=(t // tm,),
        in_specs=[pl.BlockSpec((tm, d), lambda i: (i, 0)), pl.BlockSpec((1, d), lambda i: (0, 0))],
        out_specs=pl.BlockSpec((tm, d), lambda i: (i, 0)),
        out_shape=jax.ShapeDtypeStruct((t, d), BF16),
        compiler_params=_params(("arbitrary",)), name="norm_in")(x, g)


def _mm_cols_kernel(x_ref, w_ref, o_ref, wb_ref):
    @pl.when(pl.program_id(1) == 0)
    def _():
        wb_ref[...] = _bf(w_ref[...])

    o_ref[...] = _dot(x_ref[...], wb_ref[...]).astype(o_ref.dtype)


def _mm_cols(x, w3, layer, *, tn, n_blocks, col0, tm, name):
    t, k = x.shape
    return pl.pallas_call(
        _mm_cols_kernel, grid=(n_blocks, t // tm),
        in_specs=[pl.BlockSpec((tm, k), lambda j, i: (i, 0)),
                  pl.BlockSpec((None, k, tn), lambda j, i: (layer, 0, j + col0))],
        out_specs=pl.BlockSpec((tm, tn), lambda j, i: (i, j)),
        out_shape=jax.ShapeDtypeStruct((t, n_blocks * tn), F32),
        scratch_shapes=[pltpu.VMEM((k, tn), BF16)],
        compiler_params=_params(("arbitrary", "arbitrary")), name=name)(x, w3)


def _ffn_up_kernel(x_ref, wg_ref, wu_ref, o_ref, wgb_ref, wub_ref):
    @pl.when(pl.program_id(1) == 0)
    def _():
        wgb_ref[...] = _bf(wg_ref[...])
        wub_ref[...] = _bf(wu_ref[...])

    x = x_ref[...]
    g = _dot(x, wgb_ref[...])
    u = _dot(x, wub_ref[...])
    o_ref[...] = (_silu(g) * u).astype(o_ref.dtype)


def _ffn_up(x, w_gate, w_up, layer, *, tm, tn):
    t, k = x.shape
    n = w_gate.shape[2]
    wspec = pl.BlockSpec((None, k, tn), lambda j, i: (layer, 0, j))
    return pl.pallas_call(
        _ffn_up_kernel, grid=(n // tn, t // tm),
        in_specs=[pl.BlockSpec((tm, k), lambda j, i: (i, 0)), wspec, wspec],
        out_specs=pl.BlockSpec((tm, tn), lambda j, i: (i, j)),
        out_shape=jax.ShapeDtypeStruct((t, n), BF16),
        scratch_shapes=[pltpu.VMEM((k, tn), BF16), pltpu.VMEM((k, tn), BF16)],
        compiler_params=_params(("arbitrary", "arbitrary")), name="ffn_up")(x, w_gate, w_up)


def _proj_res_kernel(x_ref, w_ref, h_ref, gpost_ref, gnext_ref, hout_ref, *rest, nk, emit_next):
    if emit_next:
        hn_ref, acc_ref = rest
    else:
        (acc_ref,) = rest
    kk = pl.program_id(1)
    part = _dot(x_ref[...], _bf(w_ref[...]))

    @pl.when(kk == 0)
    def _():
        acc_ref[...] = part

    @pl.when(kk > 0)
    def _():
        acc_ref[...] += part

    @pl.when(kk == nk - 1)
    def _():
        h = h_ref[...] + _rms(acc_ref[...], gpost_ref[...])
        hout_ref[...] = h
        if emit_next:
            hn_ref[...] = _rms(h, gnext_ref[...]).astype(hn_ref.dtype)


def _proj_res(x, w3, layer, h, g_post, g_next, *, tm, tk, emit_next, name):
    t, k = x.shape
    d = h.shape[1]
    nk = k // tk
    out_shape = [jax.ShapeDtypeStruct((t, d), F32)]
    out_specs = [pl.BlockSpec((tm, d), lambda i, kk: (i, 0))]
    if emit_next:
        out_shape.append(jax.ShapeDtypeStruct((t, d), BF16))
        out_specs.append(pl.BlockSpec((tm, d), lambda i, kk: (i, 0)))
    res = pl.pallas_call(
        functools.partial(_proj_res_kernel, nk=nk, emit_next=emit_next), grid=(t // tm, nk),
        in_specs=[pl.BlockSpec((tm, tk), lambda i, kk: (i, kk)),
                  pl.BlockSpec((None, tk, d), lambda i, kk: (layer, kk, 0)),
                  pl.BlockSpec((tm, d), lambda i, kk: (i, 0)),
                  pl.BlockSpec((1, d), lambda i, kk: (0, 0)),
                  pl.BlockSpec((1, d), lambda i, kk: (0, 0))],
        out_specs=out_specs, out_shape=out_shape,
        scratch_shapes=[pltpu.VMEM((tm, d), F32)],
        compiler_params=_params(("arbitrary", "arbitrary")), name=name)(x, w3, h, g_post, g_next)
    return (res[0], res[1]) if emit_next else (res[0], None)


def _rope(x, cos, sin):
    w = x.shape[1]
    reps = w // LANES
    cosw = jnp.tile(cos, (1, reps)) if reps > 1 else cos
    sinw = jnp.tile(sin, (1, reps)) if reps > 1 else sin
    lane = lax.broadcasted_iota(jnp.int32, x.shape, 1)
    first = (lane % DIFF_HEAD_DIM) < (DIFF_HEAD_DIM // 2)
    half = DIFF_HEAD_DIM // 2
    swapped = jnp.where(first, pltpu.roll(x, w - half, 1), pltpu.roll(x, half, 1))
    return x * cosw + swapped * sinw


def _attn_post_kernel(u_ref, gq_ref, gkv_ref, wq_ref, wuk_ref, cos_ref, sin_ref,
                      qa_ref, qr_ref, dq_ref, c_ref, kr_ref, dk_ref, dv_ref,
                      cb_ref, krb_ref, dkb_ref, dvb_ref):
    cos = cos_ref[...]
    sin = sin_ref[...]
    qn = _bf(_rms(u_ref[:, A_UQ:A_UQ + Q_LORA], gq_ref[...]))
    q = _dot(qn, _bf(wq_ref[...]))
    nope = MLA_HEADS * MLA_NOPE
    q_rope = _rope(q[:, nope:], cos, sin)
    for h in range(MLA_HEADS):
        qh = _bf(q[:, h * MLA_NOPE:(h + 1) * MLA_NOPE])
        qa_ref[h] = _dot(qh, _bf(wuk_ref[h])).astype(qa_ref.dtype)
        qr_ref[h] = q_rope[:, h * MLA_ROPE:(h + 1) * MLA_ROPE].astype(qr_ref.dtype)
    c = _rms(u_ref[:, A_UKV:A_UKV + KV_LORA], gkv_ref[...])
    c_ref[...] = c
    cb_ref[...] = _bf(c)
    kr = _rope(u_ref[:, A_UKR:A_UKR + LANES], cos, sin)[:, :MLA_ROPE]
    kr_ref[...] = kr
    krb_ref[...] = _bf(kr)
    dq = _rope(u_ref[:, A_UDQ:A_UDQ + DIFF_Q_DIM], cos, sin)
    for k in range(DIFF_KV_HEADS):
        for g in range(DIFF_GROUP):
            for p in range(2):
                c0 = ((k * DIFF_GROUP + g) * 2 + p) * DIFF_HEAD_DIM
                dq_ref[(k * 2 + p) * DIFF_GROUP + g] = dq[:, c0:c0 + DIFF_HEAD_DIM].astype(dq_ref.dtype)
    dk = _rope(u_ref[:, A_UDK:A_UDK + DIFF_K_DIM], cos, sin)
    dk_ref[...] = dk
    dkb_ref[...] = _bf(dk)
    dv = u_ref[:, A_UDV:A_UDV + DIFF_V_DIM]
    dv_ref[...] = dv
    dvb_ref[...] = _bf(dv)


def _attn_post(u, g_q, g_kv, wq, wuk_t, cos, sin, tm):
    t = u.shape[0]
    row = lambda i: (i, 0)
    const2 = lambda i: (0, 0)
    const3 = lambda i: (0, 0, 0)
    head = lambda i: (0, i, 0)
    nq = 2 * DIFF_KV_HEADS * DIFF_GROUP
    out_shape = [
        jax.ShapeDtypeStruct((MLA_HEADS, t, KV_LORA), BF16),
        jax.ShapeDtypeStruct((MLA_HEADS, t, MLA_ROPE), BF16),
        jax.ShapeDtypeStruct((nq, t, DIFF_HEAD_DIM), BF16),
        jax.ShapeDtypeStruct((t, KV_LORA), F32),
        jax.ShapeDtypeStruct((t, MLA_ROPE), F32),
        jax.ShapeDtypeStruct((t, DIFF_K_DIM), F32),
        jax.ShapeDtypeStruct((t, DIFF_V_DIM), F32),
        jax.ShapeDtypeStruct((t, KV_LORA), BF16),
        jax.ShapeDtypeStruct((t, MLA_ROPE), BF16),
        jax.ShapeDtypeStruct((t, DIFF_K_DIM), BF16),
        jax.ShapeDtypeStruct((t, DIFF_V_DIM), BF16),
    ]
    out_specs = [
        pl.BlockSpec((MLA_HEADS, tm, KV_LORA), head),
        pl.BlockSpec((MLA_HEADS, tm, MLA_ROPE), head),
        pl.BlockSpec((nq, tm, DIFF_HEAD_DIM), head),
        pl.BlockSpec((tm, KV_LORA), row), pl.BlockSpec((tm, MLA_ROPE), row),
        pl.BlockSpec((tm, DIFF_K_DIM), row), pl.BlockSpec((tm, DIFF_V_DIM), row),
        pl.BlockSpec((tm, KV_LORA), row), pl.BlockSpec((tm, MLA_ROPE), row),
        pl.BlockSpec((tm, DIFF_K_DIM), row), pl.BlockSpec((tm, DIFF_V_DIM), row),
    ]
    return pl.pallas_call(
        _attn_post_kernel, grid=(t // tm,),
        in_specs=[pl.BlockSpec((tm, u.shape[1]), row),
                  pl.BlockSpec((1, Q_LORA), const2), pl.BlockSpec((1, KV_LORA), const2),
                  pl.BlockSpec(wq.shape, const2), pl.BlockSpec(wuk_t.shape, const3),
                  pl.BlockSpec((tm, LANES), row), pl.BlockSpec((tm, LANES), row)],
        out_specs=out_specs, out_shape=out_shape,
        compiler_params=_params(("arbitrary",)), name="attn_post")(u, g_q, g_kv, wq, wuk_t, cos, sin)


def _softmax_step(s, m_ref, l_ref):
    m_old = m_ref[...]
    m_new = jnp.maximum(m_old, jnp.max(s, axis=-1, keepdims=True))
    alpha = jnp.exp(m_old - m_new)
    p = jnp.exp(s - m_new)
    l_ref[...] = alpha * l_ref[...] + jnp.sum(p, axis=-1, keepdims=True)
    m_ref[...] = m_new
    return alpha, p


def _diff_lambda(lq1_ref, lk1_ref, lq2_ref, lk2_ref, lam_init):
    return (jnp.exp(jnp.sum(lq1_ref[...] * lk1_ref[...], axis=-1, keepdims=True))
            - jnp.exp(jnp.sum(lq2_ref[...] * lk2_ref[...], axis=-1, keepdims=True)) + lam_init)


def _write_mix(o_ref, o_lat, o_diff, wuv_ref, gsub_ref, lam, lam_init, rows):
    for h in range(MLA_HEADS):
        o_ref[:, h * MLA_V:(h + 1) * MLA_V] = _dot(_bf(o_lat(h)), _bf(wuv_ref[h])).astype(o_ref.dtype)
    base = MLA_HEADS * MLA_V
    for k in range(DIFF_KV_HEADS):
        for g in range(DIFF_GROUP):
            od = o_diff(k, 0, g) - lam * o_diff(k, 1, g)
            od = _rms(od, gsub_ref[...]) * (1.0 - lam_init)
            c0 = base + (k * DIFF_GROUP + g) * DIFF_V
            o_ref[:, c0:c0 + DIFF_V] = od.astype(o_ref.dtype)


def _pattn_kernel(qa_ref, qr_ref, dq_ref, c_ref, kr_ref, dk_ref, dv_ref, wuv_ref, gsub_ref,
                  lq1_ref, lk1_ref, lq2_ref, lk2_ref, o_ref,
                  mm_ref, lm_ref, am_ref, md_ref, ld_ref, ad_ref, *, tk, lam_init):
    qi = pl.program_id(1)
    kj = pl.program_id(2)
    last = (qi * Q_BLOCK + Q_BLOCK - 1) // tk
    rows_m = MLA_HEADS * Q_BLOCK
    rows_d = DIFF_GROUP * Q_BLOCK

    @pl.when(kj == 0)
    def _():
        mm_ref[...] = jnp.full(mm_ref.shape, NEG_INF, F32)
        lm_ref[...] = jnp.zeros(lm_ref.shape, F32)
        am_ref[...] = jnp.zeros(am_ref.shape, F32)
        md_ref[...] = jnp.full(md_ref.shape, NEG_INF, F32)
        ld_ref[...] = jnp.zeros(ld_ref.shape, F32)
        ad_ref[...] = jnp.zeros(ad_ref.shape, F32)

    @pl.when(kj <= last)
    def _():
        kpos = kj * tk + lax.broadcasted_iota(jnp.int32, (1, tk), 1)
        qpos_m = qi * Q_BLOCK + (lax.broadcasted_iota(jnp.int32, (rows_m, 1), 0) % Q_BLOCK)
        c = c_ref[...]
        qa = qa_ref[...].reshape(rows_m, KV_LORA)
        qr = qr_ref[...].reshape(rows_m, MLA_ROPE)
        s = (_dot_nt(qa, c) + _dot_nt(qr, kr_ref[...])) * MLA_SCALE
        s = jnp.where(kpos <= qpos_m, s, NEG_INF)
        alpha, p = _softmax_step(s, mm_ref, lm_ref)
        am_ref[...] = alpha * am_ref[...] + _dot(_bf(p), c)

        qpos_d = qi * Q_BLOCK + (lax.broadcasted_iota(jnp.int32, (rows_d, 1), 0) % Q_BLOCK)
        allow_d = kpos <= qpos_d
        for k in range(DIFF_KV_HEADS):
            dv = dv_ref[:, k * DIFF_V:(k + 1) * DIFF_V]
            for pp in range(2):
                idx = k * 2 + pp
                q = dq_ref[idx * DIFF_GROUP:(idx + 1) * DIFF_GROUP].reshape(rows_d, DIFF_HEAD_DIM)
                kk = dk_ref[:, idx * DIFF_HEAD_DIM:(idx + 1) * DIFF_HEAD_DIM]
                sd = jnp.where(allow_d, _dot_nt(q, kk) * DIFF_SCALE, NEG_INF)
                alpha_d, pd = _softmax_step(sd, md_ref.at[idx], ld_ref.at[idx])
                ad_ref[idx] = alpha_d * ad_ref[idx] + _dot(_bf(pd), dv)

    @pl.when(kj == last)
    def _():
        lam = _diff_lambda(lq1_ref, lk1_ref, lq2_ref, lk2_ref, lam_init)

        def o_lat(h):
            sl = slice(h * Q_BLOCK, (h + 1) * Q_BLOCK)
            return am_ref[sl, :] / lm_ref[sl, :]

        def o_diff(k, pp, g):
            sl = slice(g * Q_BLOCK, (g + 1) * Q_BLOCK)
            return ad_ref[k * 2 + pp, sl, :] / ld_ref[k * 2 + pp, sl, :]

        _write_mix(o_ref, o_lat, o_diff, wuv_ref, gsub_ref, lam, lam_init, Q_BLOCK)


def _prompt_attention(qa, qr, dq, cb, krb, dkb, dvb, wuv, g_sub, lam_vecs, lam_init, batch, seq):
    tk = ATTN_TK if seq % ATTN_TK == 0 else Q_BLOCK
    nq = seq // Q_BLOCK
    nkb = seq // tk
    nd = 2 * DIFF_KV_HEADS
    qmap = lambda b, qi, kj: (0, b * nq + qi, 0)
    kmap = lambda b, qi, kj: (b * nkb + jnp.minimum(kj, (qi * Q_BLOCK + Q_BLOCK - 1) // tk), 0)
    c2 = lambda b, qi, kj: (0, 0)
    c3 = lambda b, qi, kj: (0, 0, 0)
    rows_m = MLA_HEADS * Q_BLOCK
    rows_d = DIFF_GROUP * Q_BLOCK
    return pl.pallas_call(
        functools.partial(_pattn_kernel, tk=tk, lam_init=lam_init), grid=(batch, nq, nkb),
        in_specs=[pl.BlockSpec((MLA_HEADS, Q_BLOCK, KV_LORA), qmap),
                  pl.BlockSpec((MLA_HEADS, Q_BLOCK, MLA_ROPE), qmap),
                  pl.BlockSpec((nd * DIFF_GROUP, Q_BLOCK, DIFF_HEAD_DIM), qmap),
                  pl.BlockSpec((tk, KV_LORA), kmap), pl.BlockSpec((tk, MLA_ROPE), kmap),
                  pl.BlockSpec((tk, DIFF_K_DIM), kmap), pl.BlockSpec((tk, DIFF_V_DIM), kmap),
                  pl.BlockSpec(wuv.shape, c3), pl.BlockSpec((1, DIFF_V), c2)]
                 + [pl.BlockSpec((1, DIFF_HEAD_DIM), c2)] * 4,
        out_specs=pl.BlockSpec((Q_BLOCK, ATTN_MIX), lambda b, qi, kj: (b * nq + qi, 0)),
        out_shape=jax.ShapeDtypeStruct((batch * seq, ATTN_MIX), BF16),
        scratch_shapes=[pltpu.VMEM((rows_m, 1), F32), pltpu.VMEM((rows_m, 1), F32),
                        pltpu.VMEM((rows_m, KV_LORA), F32),
                        pltpu.VMEM((nd, rows_d, 1), F32), pltpu.VMEM((nd, rows_d, 1), F32),
                        pltpu.VMEM((nd, rows_d, DIFF_V), F32)],
        compiler_params=_params(("arbitrary", "arbitrary", "arbitrary")),
        name="prompt_attn")(qa, qr, dq, cb, krb, dkb, dvb, wuv, g_sub, *lam_vecs)


SROWS = 16


def _sattn_kernel(pt_ref, qa_ref, qr_ref, dq_ref, cn_ref, krn_ref, dkn_ref, dvn_ref, *rest, n_steps):
    g_pages = SAMPLE_PAGES
    lat_refs = rest[0:g_pages]
    kr_refs = rest[g_pages:2 * g_pages]
    dk_refs = rest[2 * g_pages:3 * g_pages]
    dv_refs = rest[3 * g_pages:4 * g_pages]
    olat_ref, od_ref, mm_ref, lm_ref, am_ref, md_ref, ld_ref, ad_ref = rest[4 * g_pages:]
    del pt_ref
    j = pl.program_id(1)

    @pl.when(j == 0)
    def _():
        mm_ref[...] = jnp.full(mm_ref.shape, NEG_INF, F32)
        lm_ref[...] = jnp.zeros(lm_ref.shape, F32)
        am_ref[...] = jnp.zeros(am_ref.shape, F32)
        md_ref[...] = jnp.full(md_ref.shape, NEG_INF, F32)
        ld_ref[...] = jnp.zeros(ld_ref.shape, F32)
        ad_ref[...] = jnp.zeros(ad_ref.shape, F32)

    qa = qa_ref[...]
    qr = qr_ref[...]
    dq = dq_ref[...]
    cbs, dvbs, s_parts, sd_parts = [], [], [], []
    for g in range(g_pages):
        cb = _bf(lat_refs[g][...])
        cbs.append(cb)
        dvbs.append(_bf(dv_refs[g][...]))
        s_parts.append(_dot_nt(qa, cb) + _dot_nt(qr, _bf(kr_refs[g][...])))
        sd_parts.append(_dot_nt(dq, _bf(dk_refs[g][...])))
    s = jnp.concatenate(s_parts, axis=1) * MLA_SCALE
    alpha, p = _softmax_step(s, mm_ref, lm_ref)
    pb = _bf(p)
    pv = _dot(pb[:, 0:PAGE_SIZE], cbs[0])
    for g in range(1, g_pages):
        pv = pv + _dot(pb[:, g * PAGE_SIZE:(g + 1) * PAGE_SIZE], cbs[g])
    am_ref[...] = alpha * am_ref[...] + pv
    sd = jnp.concatenate(sd_parts, axis=1) * DIFF_SCALE
    alpha_d, pd = _softmax_step(sd, md_ref, ld_ref)
    pdb = _bf(pd)
    pvd = _dot(pdb[:, 0:PAGE_SIZE], dvbs[0])
    for g in range(1, g_pages):
        pvd = pvd + _dot(pdb[:, g * PAGE_SIZE:(g + 1) * PAGE_SIZE], dvbs[g])
    ad_ref[...] = alpha_d * ad_ref[...] + pvd

    @pl.when(j == n_steps - 1)
    def _():
        cn = _bf(cn_ref[...]).astype(F32)
        krn = _bf(krn_ref[...]).astype(F32)
        s_new = (jnp.sum(qa.astype(F32) * cn, axis=-1, keepdims=True)
                 + jnp.sum(qr.astype(F32) * krn, axis=-1, keepdims=True)) * MLA_SCALE
        a2, p2 = _softmax_step(s_new, mm_ref, lm_ref)
        acc = a2 * am_ref[...] + _bf(p2).astype(F32) * cn
        olat_ref[...] = acc / lm_ref[...]
        dkn = _bf(dkn_ref[...]).astype(F32)
        dvn = _bf(dvn_ref[...]).astype(F32)
        sd_new = jnp.sum(dq.astype(F32) * dkn, axis=-1, keepdims=True) * DIFF_SCALE
        a3, p3 = _softmax_step(sd_new, md_ref, ld_ref)
        accd = a3 * ad_ref[...] + _bf(p3).astype(F32) * dvn
        od_ref[...] = accd / ld_ref[...]


def _sample_attention(page_table, qa_s, qr_s, dq_s, c_new, kr_new, dk_new, dv_new,
                      cache_lat, cache_kr, cache_dk, cache_dv, la):
    dec, n_pages = page_table.shape
    assert n_pages % SAMPLE_PAGES == 0
    n_steps = n_pages // SAMPLE_PAGES
    seq3 = lambda s, j, pt: (s, 0, 0)

    def page_spec(width, g):
        return pl.BlockSpec((None, None, PAGE_SIZE, width),
                            lambda s, j, pt: (la, pt[s, j * SAMPLE_PAGES + g], 0, 0))

    in_specs = [pl.BlockSpec((None, SROWS, KV_LORA), seq3), pl.BlockSpec((None, SROWS, MLA_ROPE), seq3),
                pl.BlockSpec((None, SROWS, DIFF_K_DIM), seq3),
                pl.BlockSpec((None, 1, KV_LORA), seq3), pl.BlockSpec((None, 1, MLA_ROPE), seq3),
                pl.BlockSpec((None, 1, DIFF_K_DIM), seq3), pl.BlockSpec((None, 1, DIFF_V_DIM), seq3)]
    caches = []
    for cache, width in ((cache_lat, KV_LORA), (cache_kr, MLA_ROPE), (cache_dk, DIFF_K_DIM), (cache_dv, DIFF_V_DIM)):
        for g in range(SAMPLE_PAGES):
            in_specs.append(page_spec(width, g))
            caches.append(cache)
    grid_spec = pltpu.PrefetchScalarGridSpec(
        num_scalar_prefetch=1, grid=(dec, n_steps), in_specs=in_specs,
        out_specs=[pl.BlockSpec((None, SROWS, KV_LORA), seq3), pl.BlockSpec((None, SROWS, DIFF_V_DIM), seq3)],
        scratch_shapes=[pltpu.VMEM((SROWS, 1), F32), pltpu.VMEM((SROWS, 1), F32), pltpu.VMEM((SROWS, KV_LORA), F32),
                        pltpu.VMEM((SROWS, 1), F32), pltpu.VMEM((SROWS, 1), F32), pltpu.VMEM((SROWS, DIFF_V_DIM), F32)])
    return pl.pallas_call(
        functools.partial(_sattn_kernel, n_steps=n_steps), grid_spec=grid_spec,
        out_shape=[jax.ShapeDtypeStruct((dec, SROWS, KV_LORA), F32),
                   jax.ShapeDtypeStruct((dec, SROWS, DIFF_V_DIM), F32)],
        compiler_params=_params(("arbitrary", "arbitrary")),
        name="sample_attn")(page_table, qa_s, qr_s, dq_s, c_new, kr_new, dk_new, dv_new, *caches)


def _sattn_post_kernel(olat_ref, od_ref, wuv_ref, gsub_ref, lq1_ref, lk1_ref, lq2_ref, lk2_ref, o_ref,
                       *, lam_init, rows):
    lam = _diff_lambda(lq1_ref, lk1_ref, lq2_ref, lk2_ref, lam_init)
    o_lat = lambda h: olat_ref[h]
    o_diff = lambda k, pp, g: od_ref[k * 2 + pp, g * rows:(g + 1) * rows, :]
    _write_mix(o_ref, o_lat, o_diff, wuv_ref, gsub_ref, lam, lam_init, rows)


def _sample_attn_post(olat_h, od_h, wuv, g_sub, lam_vecs, lam_init):
    dec = olat_h.shape[1]
    full = lambda a: pl.BlockSpec(a.shape, lambda i: (0,) * a.ndim)
    args = (olat_h, od_h, wuv, g_sub) + tuple(lam_vecs)
    return pl.pallas_call(
        functools.partial(_sattn_post_kernel, lam_init=lam_init, rows=dec), grid=(1,),
        in_specs=[full(a) for a in args],
        out_specs=pl.BlockSpec((dec, ATTN_MIX), lambda i: (0, 0)),
        out_shape=jax.ShapeDtypeStruct((dec, ATTN_MIX), BF16),
        compiler_params=_params(("arbitrary",)), name="sample_attn_post")(*args)


def _qk_head_norm(y, o_ref, col_block, tc):
    is_q = col_block * tc < DELTA_QK_DIM
    is_qk = col_block * tc < 2 * DELTA_QK_DIM
    scale = jnp.where(is_q, DELTA_DK ** -0.5, 1.0).astype(F32)
    for hh in range(tc // DELTA_DK):
        yh = y[:, hh * DELTA_DK:(hh + 1) * DELTA_DK]
        nh = yh * lax.rsqrt(jnp.sum(yh * yh, axis=-1, keepdims=True) + EPS) * scale
        o_ref[:, hh * DELTA_DK:(hh + 1) * DELTA_DK] = jnp.where(is_qk, nh, yh)


def _delta_pre_kernel(x_ref, prev_ref, w_ref, o_ref, xp_ref, *, ts, tc, seq):
    i = pl.program_id(0)
    j = pl.program_id(1)
    x = x_ref[...]
    starts_sequence = (i * ts) % seq == 0
    xp_ref[0:8, :] = jnp.where(starts_sequence, 0.0, prev_ref[...])
    xp_ref[8:8 + ts, :] = x
    w = w_ref[...]
    y = w[3:4, :] * x
    for d in range(1, CONV_WIDTH):
        y = y + w[3 - d:4 - d, :] * xp_ref[8 - d:8 - d + ts, :]
    _qk_head_norm(_silu(y), o_ref, j, tc)


def _delta_pre(u, w_conv, ld, n_prompt, seq, ts, tc):
    sub = ts // 8
    return pl.pallas_call(
        functools.partial(_delta_pre_kernel, ts=ts, tc=tc, seq=seq), grid=(n_prompt // ts, CONV_DIM // tc),
        in_specs=[pl.BlockSpec((ts, tc), lambda i, j: (i, j)),
                  pl.BlockSpec((8, tc), lambda i, j: (jnp.maximum(i * sub - 1, 0), j)),
                  pl.BlockSpec((None, CONV_WIDTH, tc), lambda i, j: (ld, 0, j))],
        out_specs=pl.BlockSpec((ts, tc), lambda i, j: (i, j)),
        out_shape=jax.ShapeDtypeStruct((n_prompt, CONV_DIM), F32),
        scratch_shapes=[pltpu.VMEM((ts + 8, tc), F32)],
        compiler_params=_params(("arbitrary", "arbitrary")), name="delta_pre")(u, u, w_conv)


def _delta_pre_s_kernel(x_ref, buf_ref, w_ref, o_ref, nb_ref, *, tc):
    j = pl.program_id(0)
    x = x_ref[...]
    w = w_ref[...]
    y = w[3:4, :] * x
    for d in range(1, CONV_WIDTH):
        y = y + w[3 - d:4 - d, :] * buf_ref[CONV_WIDTH - 1 - d]
    nb_ref[0] = buf_ref[1]
    nb_ref[1] = buf_ref[2]
    nb_ref[2] = x
    _qk_head_norm(_silu(y), o_ref, j, tc)


def _delta_pre_sample(u, buf_t, w_conv, ld, n_prompt, dec, tc):
    rb = n_prompt // dec
    return pl.pallas_call(
        functools.partial(_delta_pre_s_kernel, tc=tc), grid=(CONV_DIM // tc,),
        in_specs=[pl.BlockSpec((dec, tc), lambda j: (rb, j)),
                  pl.BlockSpec((CONV_WIDTH - 1, dec, tc), lambda j: (0, 0, j)),
                  pl.BlockSpec((None, CONV_WIDTH, tc), lambda j: (ld, 0, j))],
        out_specs=[pl.BlockSpec((dec, tc), lambda j: (0, j)),
                   pl.BlockSpec((CONV_WIDTH - 1, dec, tc), lambda j: (0, 0, j))],
        out_shape=[jax.ShapeDtypeStruct((dec, CONV_DIM), F32),
                   jax.ShapeDtypeStruct((CONV_WIDTH - 1, dec, CONV_DIM), F32)],
        compiler_params=_params(("arbitrary",)), name="delta_pre_sample")(u, buf_t, w_conv)


def _gates_kernel(u_ref, alog_ref, dtb_ref, beta_ref, g_ref, gc_ref, *, tr):
    nh = DELTA_V_HEADS
    bb = u_ref[:, 0:nh]
    aa = u_ref[:, nh:2 * nh]
    beta_ref[...] = jax.nn.sigmoid(bb)
    x = aa + dtb_ref[...]
    softplus = jnp.maximum(x, 0.0) + jnp.log1p(jnp.exp(-jnp.abs(x)))
    g = -jnp.exp(alog_ref[...]) * softplus
    g_ref[...] = g
    ri = lax.broadcasted_iota(jnp.int32, (tr, tr), 0)
    ci = lax.broadcasted_iota(jnp.int32, (tr, tr), 1)
    tri = jnp.where((ci <= ri) & (ri // DELTA_CHUNK == ci // DELTA_CHUNK), 1.0, 0.0).astype(BF16)
    g0, g1, g2 = _split3(g)
    gc_ref[...] = _dot(tri, g0) + (_dot(tri, g1) + _dot(tri, g2))


def _delta_gates(u2, a_log, dt_bias, tr):
    t = u2.shape[0]
    nh = DELTA_V_HEADS
    row = lambda i: (i, 0)
    c2 = lambda i: (0, 0)
    return pl.pallas_call(
        functools.partial(_gates_kernel, tr=tr), grid=(t // tr,),
        in_specs=[pl.BlockSpec((tr, u2.shape[1]), row), pl.BlockSpec((1, nh), c2), pl.BlockSpec((1, nh), c2)],
        out_specs=[pl.BlockSpec((tr, nh), row)] * 3,
        out_shape=[jax.ShapeDtypeStruct((t, nh), F32)] * 3,
        compiler_params=_params(("arbitrary",)), name="delta_gates")(u2, a_log, dt_bias)


def _inv_unit_lower(n_mat, eye):
    m = -n_mat
    p = eye + m
    m2 = _dot_hi(m, m)
    steps = int(math.log2(DELTA_CHUNK)) - 1
    for it in range(steps):
        p = p + _dot_hi(p, m2)
        if it + 1 < steps:
            m2 = _dot_hi(m2, m2)
    return p


def _delta_chunk_kernel(q_ref, k_ref, v_ref, z_ref, beta_ref, gc_ref, gr_ref, gout_ref, o_ref, st_ref, s_ref,
                        *, n_chunks):
    n = pl.program_id(2)
    cs = DELTA_CHUNK
    rep = DELTA_V_HEADS // DELTA_QK_HEADS

    @pl.when(n == 0)
    def _():
        s_ref[...] = jnp.zeros(s_ref.shape, F32)

    ri = lax.broadcasted_iota(jnp.int32, (cs, cs), 0)
    ci = lax.broadcasted_iota(jnp.int32, (cs, cs), 1)
    incl = ci <= ri
    strict = ci < ri
    eye = jnp.where(ci == ri, 1.0, 0.0).astype(F32)
    for hh in range(DELTA_HG):
        qs = slice((hh // rep) * DELTA_DK, (hh // rep + 1) * DELTA_DK)
        vs = slice(hh * DELTA_DV, (hh + 1) * DELTA_DV)
        q = q_ref[:, qs]
        k = k_ref[:, qs]
        v = v_ref[:, vs]
        beta = beta_ref[:, hh:hh + 1]
        gc = gc_ref[:, hh:hh + 1]
        gr = gr_ref[hh:hh + 1, :]
        gdiff = gc - gr
        decay = jnp.where(incl, jnp.exp(jnp.where(incl, gdiff, 0.0)), 0.0)
        kb = k * beta
        kbf = _bf(k)
        lmat = jnp.where(strict, _dot_nt(_bf(kb), kbf) * decay, 0.0)
        tinv = _inv_unit_lower(lmat, eye)
        egc = jnp.exp(gc)
        solved = _dot_hi(tinv, jnp.concatenate([v * beta, kb * egc], axis=1))
        value = solved[:, :DELTA_DV]
        kcd = solved[:, DELTA_DV:]
        att = _dot_nt(_bf(q), kbf) * decay
        g_last = gc[cs - 1:cs, :]
        q_dec = q * egc
        k_dec = k * jnp.exp(g_last - gc)
        st = s_ref[hh]
        stb = _bf(st)
        v_new = value - _dot(_bf(kcd), stb)
        vnb = _bf(v_new)
        o = _dot(_bf(q_dec), stb) + _dot(_bf(att), vnb)
        s_ref[hh] = st * jnp.exp(g_last) + _dot(_bf(k_dec.T), vnb)
        z = z_ref[:, vs]
        o_ref[:, vs] = (_rms(o, gout_ref[...]) * _silu(z)).astype(o_ref.dtype)

    @pl.when(n == n_chunks - 1)
    def _():
        st_ref[...] = s_ref[...]


def _delta_chunked(qkv, u, beta_c, gc_c, gc_r, g_out, batch, seq):
    n_chunks = seq // DELTA_CHUNK
    nhg = DELTA_V_HEADS // DELTA_HG
    rep = DELTA_V_HEADS // DELTA_QK_HEADS
    qw = DELTA_HG // rep * DELTA_DK
    vw = DELTA_HG * DELTA_DV
    k0 = DELTA_QK_DIM // qw
    v0 = 2 * DELTA_QK_DIM // vw
    z0 = CONV_DIM // vw
    row = lambda b, hg, n: b * n_chunks + n
    return pl.pallas_call(
        functools.partial(_delta_chunk_kernel, n_chunks=n_chunks), grid=(batch, nhg, n_chunks),
        in_specs=[pl.BlockSpec((DELTA_CHUNK, qw), lambda b, hg, n: (row(b, hg, n), hg)),
                  pl.BlockSpec((DELTA_CHUNK, qw), lambda b, hg, n: (row(b, hg, n), k0 + hg)),
                  pl.BlockSpec((DELTA_CHUNK, vw), lambda b, hg, n: (row(b, hg, n), v0 + hg)),
                  pl.BlockSpec((DELTA_CHUNK, vw), lambda b, hg, n: (row(b, hg, n), z0 + hg)),
                  pl.BlockSpec((None, DELTA_CHUNK, DELTA_HG), lambda b, hg, n: (hg, row(b, hg, n), 0)),
                  pl.BlockSpec((None, DELTA_CHUNK, DELTA_HG), lambda b, hg, n: (hg, row(b, hg, n), 0)),
                  pl.BlockSpec((None, None, DELTA_HG, DELTA_CHUNK), lambda b, hg, n: (hg, row(b, hg, n), 0, 0)),
                  pl.BlockSpec((1, DELTA_DV), lambda b, hg, n: (0, 0))],
        out_specs=[pl.BlockSpec((DELTA_CHUNK, vw), lambda b, hg, n: (row(b, hg, n), hg)),
                   pl.BlockSpec((None, DELTA_HG, DELTA_DK, DELTA_DV), lambda b, hg, n: (b, hg, 0, 0))],
        out_shape=[jax.ShapeDtypeStruct((batch * seq, DELTA_V_DIM), BF16),
                   jax.ShapeDtypeStruct((batch, DELTA_V_HEADS, DELTA_DK, DELTA_DV), F32)],
        scratch_shapes=[pltpu.VMEM((DELTA_HG, DELTA_DK, DELTA_DV), F32)],
        compiler_params=_params(("arbitrary", "arbitrary", "arbitrary")),
        name="delta_chunk")(qkv, qkv, qkv, u, beta_c, gc_c, gc_r, g_out)


def _delta_step_kernel(qk_ref, v_ref, z_ref, beta_ref, g_ref, gout_ref, s_ref, o_ref, ns_ref):
    rep = DELTA_V_HEADS // DELTA_QK_HEADS
    nqk = STEP_HB // rep
    for hh in range(STEP_HB):
        qc = qk_ref[:, hh // rep:hh // rep + 1]
        kc = qk_ref[:, nqk + hh // rep:nqk + hh // rep + 1]
        vs = slice(hh * DELTA_DV, (hh + 1) * DELTA_DV)
        v = v_ref[:, vs]
        beta = beta_ref[:, hh:hh + 1]
        eg = jnp.exp(g_ref[:, hh:hh + 1])
        st = s_ref[hh]
        ks = jnp.sum(kc * st, axis=0, keepdims=True)
        qs = jnp.sum(qc * st, axis=0, keepdims=True)
        qk = jnp.sum(qc * kc, axis=0, keepdims=True)
        v_new = beta * v - (beta * eg) * ks
        o = eg * qs + qk * v_new
        ns_ref[hh] = st * eg + kc * v_new
        o_ref[:, vs] = _rms(o, gout_ref[...]) * _silu(z_ref[:, vs])


def _delta_step(qk_col, u3, qkv3, beta_s, g_s, g_out, state, ld, n_prompt):
    dec = qk_col.shape[0]
    nhb = DELTA_V_HEADS // STEP_HB
    vw = STEP_HB * DELTA_DV
    v0 = 2 * DELTA_QK_DIM // vw
    z0 = CONV_DIM // vw
    return pl.pallas_call(
        _delta_step_kernel, grid=(dec, nhb),
        in_specs=[pl.BlockSpec((None, None, DELTA_DK, STEP_HB), lambda s, hb: (s, hb, 0, 0)),
                  pl.BlockSpec((None, 1, vw), lambda s, hb: (s, 0, v0 + hb)),
                  pl.BlockSpec((None, 1, vw), lambda s, hb: (n_prompt + s, 0, z0 + hb)),
                  pl.BlockSpec((None, None, 1, STEP_HB), lambda s, hb: (s, hb, 0, 0)),
                  pl.BlockSpec((None, None, 1, STEP_HB), lambda s, hb: (s, hb, 0, 0)),
                  pl.BlockSpec((1, DELTA_DV), lambda s, hb: (0, 0)),
                  pl.BlockSpec((None, None, STEP_HB, DELTA_DK, DELTA_DV), lambda s, hb: (ld, s, hb, 0, 0))],
        out_specs=[pl.BlockSpec((None, 1, vw), lambda s, hb: (s, 0, hb)),
                   pl.BlockSpec((None, STEP_HB, DELTA_DK, DELTA_DV), lambda s, hb: (s, hb, 0, 0))],
        out_shape=[jax.ShapeDtypeStruct((dec, 1, DELTA_V_DIM), F32),
                   jax.ShapeDtypeStruct((dec, DELTA_V_HEADS, DELTA_DK, DELTA_DV), F32)],
        compiler_params=_params(("arbitrary", "arbitrary")),
        name="delta_step")(qk_col, qkv3, u3, beta_s, g_s, g_out, state)


def _rope_tables(pos):
    d = DIFF_HEAD_DIM
    inv = jnp.power(jnp.float32(ROPE_THETA), -jnp.arange(0, d, 2, dtype=F32) / d)
    ang = pos.astype(F32)[:, None] * inv[None, :]
    cos, sin = jnp.cos(ang), jnp.sin(ang)
    return jnp.tile(jnp.concatenate([cos, cos], axis=1), (1, 2)), jnp.tile(jnp.concatenate([-sin, sin], axis=1), (1, 2))


def _attention_layer(hn, cos, sin, layer, la, page_table, caches, wts, dims, tm):
    batch, seq, dec = dims
    n_prompt = batch * seq
    (w_in_attn, g_q_lat, g_kv_lat, w_q_up, w_kv_up, lam_vecs, g_diff_sub) = wts
    w = w_in_attn[la]
    o2 = Q_LORA + KV_LORA
    o3 = o2 + MLA_ROPE
    w_perm = jnp.concatenate(
        [w[:, :o2], w[:, o3:], w[:, o2:o3], jnp.zeros((w.shape[0], ATTN_IN_PAD - w.shape[1]), w.dtype)], axis=1)
    u = _mm_cols(hn, w_perm[None], 0, tn=ATTN_IN_TN, n_blocks=ATTN_IN_PAD // ATTN_IN_TN, col0=0, tm=tm,
                 name="attn_in")
    wq3 = w_q_up[la].reshape(Q_LORA, MLA_HEADS, MLA_NOPE + MLA_ROPE)
    wq = jnp.concatenate([wq3[:, :, :MLA_NOPE].reshape(Q_LORA, -1), wq3[:, :, MLA_NOPE:].reshape(Q_LORA, -1)], axis=1)
    wkv3 = w_kv_up[la].reshape(KV_LORA, MLA_HEADS, MLA_NOPE + MLA_V)
    wuk_t = jnp.transpose(wkv3[:, :, :MLA_NOPE], (1, 2, 0))
    wuv = jnp.transpose(wkv3[:, :, MLA_NOPE:], (1, 0, 2))
    (qa, qr, dq, c, kr, dk, dv, cb, krb, dkb, dvb) = _attn_post(
        u, g_q_lat[la][None], g_kv_lat[la][None], wq, wuk_t, cos, sin, tm)
    lam_init = 0.8 - 0.6 * math.exp(-0.3 * layer)
    lams = tuple(v[la][None] for v in lam_vecs)
    g_sub = g_diff_sub[la][None]
    mix_p = _prompt_attention(qa, qr, dq, cb, krb, dkb, dvb, wuv, g_sub, lams, lam_init, batch, seq)
    pad = SROWS - MLA_HEADS
    qa_s = jnp.pad(jnp.transpose(qa[:, n_prompt:], (1, 0, 2)), ((0, 0), (0, pad), (0, 0)))
    qr_s = jnp.pad(jnp.transpose(qr[:, n_prompt:], (1, 0, 2)), ((0, 0), (0, pad), (0, 0)))
    dq_t = jnp.transpose(dq[:, n_prompt:], (1, 0, 2))
    nkp = 2 * DIFF_KV_HEADS
    onehot = jnp.repeat(jnp.eye(nkp, dtype=dq_t.dtype), DIFF_GROUP, axis=0)
    dq_s = (dq_t[:, :, None, :] * onehot[None, :, :, None]).reshape(dec, SROWS, DIFF_K_DIM)
    cache_lat, cache_kr, cache_dk, cache_dv = caches
    olat, od = _sample_attention(
        page_table, qa_s, qr_s, dq_s,
        c[n_prompt:, None], kr[n_prompt:, None], dk[n_prompt:, None], dv[n_prompt:, None],
        cache_lat, cache_kr, cache_dk.reshape(cache_dk.shape[:3] + (DIFF_K_DIM,)),
        cache_dv.reshape(cache_dv.shape[:3] + (DIFF_V_DIM,)), la)
    olat_h = jnp.transpose(olat[:, :MLA_HEADS], (1, 0, 2))
    od6 = od.reshape(dec, DIFF_KV_HEADS, 2, DIFF_GROUP, DIFF_KV_HEADS, DIFF_V)
    od_diag = jnp.stack([od6[:, k, :, :, k, :] for k in range(DIFF_KV_HEADS)], axis=0)
    od_h = jnp.transpose(od_diag, (0, 2, 3, 1, 4)).reshape(nkp, DIFF_GROUP * dec, DIFF_V)
    mix_s = _sample_attn_post(olat_h, od_h, wuv, g_sub, lams, lam_init)
    mix = jnp.concatenate([mix_p, mix_s], axis=0)
    return mix, (c, kr, dk, dv)


def _delta_layer(hn, ld, state_delta, state_conv, wts, dims, tm):
    batch, seq, dec = dims
    n_prompt = batch * seq
    t = hn.shape[0]
    (w_in_delta, w_conv, a_log, dt_bias, g_delta_out) = wts
    main = CONV_DIM + DELTA_V_DIM
    tn = 512
    u = _mm_cols(hn, w_in_delta, ld, tn=tn, n_blocks=main // tn, col0=0, tm=tm, name="delta_in")
    w_gates = jnp.pad(w_in_delta[ld][:, main:], ((0, 0), (0, LANES - 2 * DELTA_V_HEADS)))
    u2 = _mm_cols(hn, w_gates[None], 0, tn=LANES, n_blocks=1, col0=0, tm=tm, name="delta_in_gates")
    beta, g, gc = _delta_gates(u2, a_log[ld][None], dt_bias[ld][None], LANES)
    g_out = g_delta_out[ld][None]
    ts = _row_tile(seq, 512, 8)
    qkv_p = _delta_pre(u, w_conv, ld, n_prompt, seq, ts, 1024)
    nhg = DELTA_V_HEADS // DELTA_HG
    hg_cols = lambda a: jnp.transpose(a.reshape(a.shape[0], nhg, DELTA_HG), (1, 0, 2))
    beta_c = hg_cols(beta[:n_prompt])
    gc_c = hg_cols(gc[:n_prompt])
    gc_r = jnp.transpose(gc_c.reshape(nhg, n_prompt // DELTA_CHUNK, DELTA_CHUNK, DELTA_HG), (0, 1, 3, 2))
    omix_p, st_p = _delta_chunked(qkv_p, u, beta_c, gc_c, gc_r, g_out, batch, seq)
    buf_p = u[:n_prompt, :CONV_DIM].reshape(batch, seq, CONV_DIM)[:, seq - (CONV_WIDTH - 1):]
    buf_t = jnp.transpose(state_conv[ld], (1, 0, 2))
    qkv_s, nb_t = _delta_pre_sample(u, buf_t, w_conv, ld, n_prompt, dec, 1024)
    buf_s = jnp.transpose(nb_t, (1, 0, 2))
    rep = DELTA_V_HEADS // DELTA_QK_HEADS
    nhb = DELTA_V_HEADS // STEP_HB
    nqk = STEP_HB // rep
    q_s = qkv_s[:, :DELTA_QK_DIM].reshape(dec, nhb, nqk, DELTA_DK)
    k_s = qkv_s[:, DELTA_QK_DIM:2 * DELTA_QK_DIM].reshape(dec, nhb, nqk, DELTA_DK)
    qk_col = jnp.transpose(jnp.concatenate([q_s, k_s], axis=2), (0, 1, 3, 2))
    beta_s = beta[n_prompt:].reshape(dec, nhb, 1, STEP_HB)
    g_s = g[n_prompt:].reshape(dec, nhb, 1, STEP_HB)
    o_s, st_s = _delta_step(qk_col, u.reshape(t, 1, main), qkv_s.reshape(dec, 1, CONV_DIM), beta_s, g_s, g_out,
                            state_delta, ld, n_prompt)
    omix = jnp.concatenate([omix_p, o_s.reshape(dec, DELTA_V_DIM).astype(BF16)], axis=0)
    return omix, (st_p, st_s, buf_p, buf_s)


def kernel(x_prompt, x_sample, cache_mla_latent, cache_mla_krope, cache_diff_k, cache_diff_v, state_delta, state_conv, page_table, g_mix_pre, g_mix_post, g_ffn_pre, g_ffn_post, w_in_attn, g_q_lat, g_kv_lat, w_q_up, w_kv_up, lambda_q1, lambda_k1, lambda_q2, lambda_k2, g_diff_sub, w_out_attn, w_in_delta, w_conv, a_log, dt_bias, g_delta_out, w_out_delta, w_gate, w_up, w_down):
    batch, seq, d_model = x_prompt.shape
    dec, dec_seq, _ = x_sample.shape
    assert dec_seq == 1, "sample requests carry one new token each"
    depth = g_mix_pre.shape[0]
    n_prompt = batch * seq
    t = n_prompt + dec
    assert seq % Q_BLOCK == 0 and seq % DELTA_CHUNK == 0 and n_prompt % dec == 0 and t % LANES == 0
    dims = (batch, seq, dec)
    tm = _row_tile(t, 704)
    past_len = page_table.shape[1] * PAGE_SIZE
    pos = jnp.concatenate([jnp.tile(jnp.arange(seq, dtype=jnp.int32), batch),
                           jnp.full((dec,), past_len, jnp.int32)])
    cos, sin = _rope_tables(pos)

    h = jnp.concatenate([x_prompt.reshape(n_prompt, d_model), x_sample.reshape(dec, d_model)], axis=0)
    hn = _norm_bf16(h, g_mix_pre[0][None], tm)
    attn_out, delta_out = [], []
    for l in range(depth):
        if l % 2 == 0:
            la = l // 2
            wts = (w_in_attn, g_q_lat, g_kv_lat, w_q_up, w_kv_up,
                   (lambda_q1, lambda_k1, lambda_q2, lambda_k2), g_diff_sub)
            mix, new_rows = _attention_layer(
                hn, cos, sin, l, la, page_table,
                (cache_mla_latent, cache_mla_krope, cache_diff_k, cache_diff_v), wts, dims, tm)
            attn_out.append(new_rows)
            w_out, lw = w_out_attn, la
        else:
            ld = l // 2
            wts = (w_in_delta, w_conv, a_log, dt_bias, g_delta_out)
            mix, new_state = _delta_layer(hn, ld, state_delta, state_conv, wts, dims, tm)
            delta_out.append(new_state)
            w_out, lw = w_out_delta, ld
        h, hn = _proj_res(mix, w_out, lw, h, g_mix_post[l][None], g_ffn_pre[l][None],
                          tm=tm, tk=512, emit_next=True, name="mix_out")
        act = _ffn_up(hn, w_gate, w_up, l, tm=tm, tn=512)
        last = l == depth - 1
        g_next = g_mix_pre[l + 1][None] if not last else g_ffn_post[l][None]
        h, hn = _proj_res(act, w_down, l, h, g_ffn_post[l][None], g_next,
                          tm=tm, tk=512, emit_next=not last, name="ffn_down")

    def split(rows, tail):
        a = jnp.stack(rows)
        return (a[:, :n_prompt].reshape((len(rows), batch, seq) + tail),
                a[:, n_prompt:].reshape((len(rows), dec, 1) + tail))

    lat_p, lat_s = split([r[0] for r in attn_out], (KV_LORA,))
    kr_p, kr_s = split([r[1] for r in attn_out], (MLA_ROPE,))
    dk_p, dk_s = split([r[2] for r in attn_out], (DIFF_KV_HEADS, 2, DIFF_HEAD_DIM))
    dv_p, dv_s = split([r[3] for r in attn_out], (DIFF_KV_HEADS, DIFF_V))
    return (h[:n_prompt].reshape(batch, seq, d_model), h[n_prompt:].reshape(dec, 1, d_model),
            lat_p, lat_s, kr_p, kr_s, dk_p, dk_s, dv_p, dv_s,
            jnp.stack([s[0] for s in delta_out]), jnp.stack([s[1] for s in delta_out]),
            jnp.stack([s[2] for s in delta_out]), jnp.stack([s[3] for s in delta_out]))
```
